```python
import jax, jax.numpy as jnp
from jax import lax
import numpy as np

D_MODEL = 1024
BATCH = 16
SEQ = 2048
DEPTH = 2

N_MIXERS = 2
N_MLA_LAYERS = (DEPTH + 1) // 2
N_FOX_LAYERS = DEPTH // 2

MLA_HEADS = 8
MLA_NOPE_DIM = 128
MLA_ROPE_DIM = 64
MLA_V_DIM = 128
MLA_Q_RANK = 256
MLA_KV_RANK = 256
ROPE_THETA = 10000.0

FOX_HEADS = 16
FOX_HEAD_DIM = D_MODEL // FOX_HEADS

D_FF = -(-8 * D_MODEL // (3 * 256)) * 256

Q_BLOCK = 128
DEEPNORM_ALPHA = (2.0 * DEPTH) ** 0.25
DEEPNORM_BETA = (8.0 * DEPTH) ** -0.25
NORM_EPS = 1e-5
MLA_IN_DIM = MLA_Q_RANK + MLA_KV_RANK + MLA_ROPE_DIM
FOX_IN_DIM = 3 * D_MODEL + FOX_HEADS

kernel_name = "hybrid_mla_fox_deepnorm_adaln"


def rms_norm(x, g):
    xf = x.astype(jnp.float32)
    y = xf * lax.rsqrt(jnp.mean(xf * xf, axis=-1, keepdims=True) + NORM_EPS)
    return (y * g.astype(jnp.float32)).astype(x.dtype)


def layer_norm(x, g, b):
    xf = x.astype(jnp.float32)
    mu = jnp.mean(xf, axis=-1, keepdims=True)
    var = jnp.mean(jnp.square(xf - mu), axis=-1, keepdims=True)
    y = (xf - mu) * lax.rsqrt(var + NORM_EPS)
    return (y * g.astype(jnp.float32) + b.astype(jnp.float32)).astype(x.dtype)


def rotary_angles(positions, dim):
    half = dim // 2
    inv_freq = ROPE_THETA ** (-jnp.arange(half, dtype=jnp.float32) / half)
    ang = positions.astype(jnp.float32)[..., None] * inv_freq
    return jnp.cos(ang), jnp.sin(ang)


def apply_rotary(x, cos, sin):
    half = x.shape[-1] // 2
    x1, x2 = x[..., :half], x[..., half:]
    cos = cos.astype(x.dtype)
    sin = sin.astype(x.dtype)
    return jnp.concatenate([x1 * cos - x2 * sin, x2 * cos + x1 * sin], axis=-1)


def causal_block_attention(logits_fn, v):
    b, h, s, dv = v.shape
    key_pos = jnp.arange(s)

    def one_block(blk):
        q_start = blk * Q_BLOCK
        logits = logits_fn(q_start)
        q_pos = q_start + jnp.arange(Q_BLOCK)
        causal = q_pos[:, None] >= key_pos[None, :]
        probs = jax.nn.softmax(jnp.where(causal, logits, -jnp.inf), axis=-1)
        return jnp.einsum('bhqs,bhsd->bhqd', probs.astype(v.dtype), v)

    out = lax.map(one_block, jnp.arange(s // Q_BLOCK))
    return out.transpose(1, 0, 3, 2, 4).reshape(b, s, h * dv)


def mla_mixer(u, cos, sin, w_in, g_q, w_uq, g_kv, w_uk, w_uv, w_o):
    b, s, _ = u.shape
    h_in = u @ w_in
    c_q = rms_norm(h_in[..., :MLA_Q_RANK], g_q)
    c_kv = rms_norm(h_in[..., MLA_Q_RANK:MLA_Q_RANK + MLA_KV_RANK], g_kv)
    k_rope = apply_rotary(h_in[..., MLA_Q_RANK + MLA_KV_RANK:], cos, sin)

    q = (c_q @ w_uq).reshape(b, s, MLA_HEADS, MLA_NOPE_DIM + MLA_ROPE_DIM)
    q_nope = q[..., :MLA_NOPE_DIM].transpose(0, 2, 1, 3)
    q_rope = apply_rotary(q[..., MLA_NOPE_DIM:], cos[:, :, None, :], sin[:, :, None, :])
    q_rope = q_rope.transpose(0, 2, 1, 3)
    k_nope = (c_kv @ w_uk).reshape(b, s, MLA_HEADS, MLA_NOPE_DIM).transpose(0, 2, 1, 3)
    v = (c_kv @ w_uv).reshape(b, s, MLA_HEADS, MLA_V_DIM).transpose(0, 2, 1, 3)
    scale = (MLA_NOPE_DIM + MLA_ROPE_DIM) ** -0.5

    def logits_fn(q_start):
        qn = lax.dynamic_slice_in_dim(q_nope, q_start, Q_BLOCK, axis=2)
        qr = lax.dynamic_slice_in_dim(q_rope, q_start, Q_BLOCK, axis=2)
        sc = (jnp.einsum('bhqd,bhsd->bhqs', qn, k_nope)
              + jnp.einsum('bhqr,bsr->bhqs', qr, k_rope))
        return sc.astype(jnp.float32) * scale

    return causal_block_attention(logits_fn, v) @ w_o


def fox_mixer(u, w_in, b_f, w_o):
    b, s, d = u.shape
    h_in = u @ w_in
    def heads(t):
        return t.reshape(b, s, FOX_HEADS, FOX_HEAD_DIM).transpose(0, 2, 1, 3)
    q = heads(h_in[..., :d])
    k = heads(h_in[..., d:2 * d])
    v = heads(h_in[..., 2 * d:3 * d])
    log_f = jax.nn.log_sigmoid(h_in[..., 3 * d:].astype(jnp.float32) + b_f.astype(jnp.float32))
    cum_log_f = lax.cumsum(log_f, axis=1).transpose(0, 2, 1)
    scale = FOX_HEAD_DIM ** -0.5

    def logits_fn(q_start):
        qb = lax.dynamic_slice_in_dim(q, q_start, Q_BLOCK, axis=2)
        fq = lax.dynamic_slice_in_dim(cum_log_f, q_start, Q_BLOCK, axis=2)
        sc = jnp.einsum('bhqd,bhsd->bhqs', qb, k).astype(jnp.float32) * scale
        return sc + fq[..., :, None] - cum_log_f[:, :, None, :]

    return causal_block_attention(logits_fn, v) @ w_o


def swiglu(u, w_gate, w_up, w_down):
    return (jax.nn.silu(u @ w_gate) * (u @ w_up)) @ w_down


def modulate(x, shift, scale):
    return x * (1.0 + scale[:, None, :]) + shift[:, None, :]


def setup_inputs(seed: int = 0) -> dict:
    key = jax.random.key(seed)
    ks = iter(jax.random.split(key, 40))
    f32 = jnp.float32
    def nrm(shape, std):
        return jax.random.normal(next(ks), shape, f32) * std
    D, H, Hf = D_MODEL, MLA_HEADS, FOX_HEADS
    beta = DEEPNORM_BETA
    nm, nf = N_MLA_LAYERS, N_FOX_LAYERS

    x = jax.random.normal(next(ks), (BATCH, SEQ, D), f32)
    c = jax.random.normal(next(ks), (BATCH, D), f32)
    positions = (jnp.arange(SEQ, dtype=jnp.int32)[None, :]
                 + jax.random.randint(next(ks), (BATCH, 1), 0, 128, dtype=jnp.int32))

    mla_w_in = nrm((nm, D, MLA_IN_DIM), D ** -0.5)
    mla_g_q = 1.0 + nrm((nm, MLA_Q_RANK), 0.02)
    mla_w_uq = nrm((nm, MLA_Q_RANK, H * (MLA_NOPE_DIM + MLA_ROPE_DIM)), MLA_Q_RANK ** -0.5)
    mla_g_kv = 1.0 + nrm((nm, MLA_KV_RANK), 0.02)
    mla_w_uk = nrm((nm, MLA_KV_RANK, H * MLA_NOPE_DIM), MLA_KV_RANK ** -0.5)
    mla_w_uv = nrm((nm, MLA_KV_RANK, H * MLA_V_DIM), beta * MLA_KV_RANK ** -0.5)
    mla_w_o = nrm((nm, H * MLA_V_DIM, D), beta * (H * MLA_V_DIM) ** -0.5)

    fox_w_in = jnp.concatenate([
        nrm((nf, D, 2 * D), D ** -0.5),
        nrm((nf, D, D), beta * D ** -0.5),
        nrm((nf, D, Hf), D ** -0.5),
    ], axis=-1)
    fox_b_f = 2.0 + nrm((nf, Hf), 0.5)
    fox_w_o = nrm((nf, D, D), beta * D ** -0.5)

    ada_w = nrm((DEPTH, D, 6 * D), 0.1 * D ** -0.5)
    ada_b = nrm((DEPTH, 6 * D), 0.02)

    ffn_w_gate = nrm((DEPTH, D, D_FF), beta * D ** -0.5)
    ffn_w_up = nrm((DEPTH, D, D_FF), beta * D ** -0.5)
    ffn_w_down = nrm((DEPTH, D_FF, D), beta * D_FF ** -0.5)

    ln_g = 1.0 + nrm((DEPTH, 2, D), 0.02)
    ln_b = nrm((DEPTH, 2, D), 0.02)

    return {"x": x, "c": c, "positions": positions,
            "mla_w_in": mla_w_in, "mla_g_q": mla_g_q, "mla_w_uq": mla_w_uq,
            "mla_g_kv": mla_g_kv, "mla_w_uk": mla_w_uk, "mla_w_uv": mla_w_uv, "mla_w_o": mla_w_o,
            "fox_w_in": fox_w_in, "fox_b_f": fox_b_f, "fox_w_o": fox_w_o,
            "ada_w": ada_w, "ada_b": ada_b,
            "ffn_w_gate": ffn_w_gate, "ffn_w_up": ffn_w_up, "ffn_w_down": ffn_w_down,
            "ln_g": ln_g, "ln_b": ln_b}


def reference(x, c, positions, mla_w_in, mla_g_q, mla_w_uq, mla_g_kv, mla_w_uk, mla_w_uv, mla_w_o,
              fox_w_in, fox_b_f, fox_w_o, ada_w, ada_b, ffn_w_gate, ffn_w_up, ffn_w_down,
              ln_g, ln_b):
    d = D_MODEL
    cos, sin = rotary_angles(positions, MLA_ROPE_DIM)
    c_act = jax.nn.silu(c)
    for i in range(DEPTH):
        mod = c_act @ ada_w[i] + ada_b[i]
        sh_a, sc_a, gt_a = mod[:, :d], mod[:, d:2 * d], mod[:, 2 * d:3 * d]
        sh_f, sc_f, gt_f = mod[:, 3 * d:4 * d], mod[:, 4 * d:5 * d], mod[:, 5 * d:]

        u = modulate(x, sh_a, sc_a)
        j = i // N_MIXERS
        if i % N_MIXERS == 0:
            y = mla_mixer(u, cos, sin, mla_w_in[j], mla_g_q[j], mla_w_uq[j], mla_g_kv[j],
                          mla_w_uk[j], mla_w_uv[j], mla_w_o[j])
        else:
            y = fox_mixer(u, fox_w_in[j], fox_b_f[j], fox_w_o[j])
        x = layer_norm(DEEPNORM_ALPHA * x + (1.0 + gt_a[:, None, :]) * y, ln_g[i, 0], ln_b[i, 0])

        u = modulate(x, sh_f, sc_f)
        y = swiglu(u, ffn_w_gate[i], ffn_w_up[i], ffn_w_down[i])
        x = layer_norm(DEEPNORM_ALPHA * x + (1.0 + gt_f[:, None, :]) * y, ln_g[i, 1], ln_b[i, 1])
    return x
```

```python
import functools
import math

import jax
import jax.numpy as jnp
from jax import lax
from jax.experimental import pallas as pl
from jax.experimental.pallas import tpu as pltpu

D_MODEL = 1024
MLA_HEADS = 8
MLA_NOPE_DIM = 128
MLA_ROPE_DIM = 64
MLA_V_DIM = 128
MLA_Q_RANK = 256
MLA_KV_RANK = 256
ROPE_THETA = 10000.0
FOX_HEADS = 16
FOX_HEAD_DIM = D_MODEL // FOX_HEADS
N_MIXERS = 2
NORM_EPS = 1e-5

LANES = 128
ROPE_HALF = MLA_ROPE_DIM // 2
FOX_GROUP = 16
FOX_PAIRS = FOX_HEADS // 2

TOKEN_TILE = 512
ATTN_TILE = 512
FFN_CHUNK = 256
CUMSUM_BLOCK = 128
VMEM_LIMIT = 56 * 1024 * 1024

F32 = jnp.float32
BF16 = jnp.bfloat16


def _dot(a, b):
    return jnp.dot(a, b, preferred_element_type=F32)


def _dot_nt(a, b):
    return lax.dot_general(a, b, (((1,), (1,)), ((), ())), preferred_element_type=F32)


def _layer_norm(z, g, b):
    mu = jnp.mean(z, axis=-1, keepdims=True)
    zc = z - mu
    var = jnp.mean(zc * zc, axis=-1, keepdims=True)
    return zc * lax.rsqrt(var + NORM_EPS) * g + b


def _rms_norm(h, g):
    return h * lax.rsqrt(jnp.mean(h * h, axis=-1, keepdims=True) + NORM_EPS) * g


def _sigmoid(x):
    return 1.0 / (1.0 + jnp.exp(-x))


def _split3(x):
    hi = x.astype(BF16)
    r1 = x - hi.astype(F32)
    mid = r1.astype(BF16)
    lo = (r1 - mid.astype(F32)).astype(BF16)
    return hi, mid, lo


def _const_spec(shape):
    nd = len(shape)
    return pl.BlockSpec(shape, lambda *_: (0,) * nd, pipeline_mode=pl.Buffered(1))


def _params(*sem):
    return pltpu.CompilerParams(dimension_semantics=sem, vmem_limit_bytes=VMEM_LIMIT)


def _adaln_kernel(c_ref, w_ref, b_ref, o_ref):
    c = c_ref[...]
    c_act = (c * _sigmoid(c)).astype(BF16)
    o_ref[...] = _dot(c_act, w_ref[...].astype(BF16)) + b_ref[...]


def _adaln(c, ada_w, ada_b):
    depth, d, n = ada_w.shape
    b = c.shape[0]
    tn = 1024
    return pl.pallas_call(
        _adaln_kernel,
        grid=(depth, n // tn),
        in_specs=[
            pl.BlockSpec((b, d), lambda i, j: (0, 0)),
            pl.BlockSpec((None, d, tn), lambda i, j: (i, 0, j)),
            pl.BlockSpec((None, 1, tn), lambda i, j: (i, 0, j)),
        ],
        out_specs=pl.BlockSpec((None, b, tn), lambda i, j: (i, 0, j)),
        out_shape=jax.ShapeDtypeStruct((depth, b, n), F32),
        compiler_params=_params("parallel", "parallel"),
        name="adaln",
    )(c, ada_w, ada_b.reshape(depth, 1, n))


def _rope_pairs(x, cos4, sin4s):
    blocks = []
    for j in range(x.shape[1] // LANES):
        xb = x[:, j * LANES:(j + 1) * LANES]
        blocks.append(xb * cos4 + pltpu.roll(xb, LANES // 2, 1) * sin4s)
    return blocks[0] if len(blocks) == 1 else jnp.concatenate(blocks, axis=1)


def _mla_proj_kernel(x_ref, mod_ref, pos_ref, freq_ref, w_in_ref, gq_ref, gkv_ref,
                     w_qn_ref, w_qr_ref, w_uk_ref, w_uvt_ref,
                     qn_ref, qr_ref, kn_ref, kr_ref, vt_ref):
    x = x_ref[...]
    u = (x * (1.0 + mod_ref[1:2, :]) + mod_ref[0:1, :]).astype(BF16)
    h = _dot(u, w_in_ref[...])
    c_q = _rms_norm(h[:, :MLA_Q_RANK], gq_ref[...]).astype(BF16)
    c_kv = _rms_norm(h[:, MLA_Q_RANK:MLA_Q_RANK + MLA_KV_RANK], gkv_ref[...]).astype(BF16)

    ang = pos_ref[...].astype(F32) * freq_ref[...]
    cos4 = jnp.cos(ang)
    sin = jnp.sin(ang)
    lane = lax.broadcasted_iota(jnp.int32, (1, LANES), 1)
    sin4s = jnp.where(lane < LANES // 2, -sin, sin)

    kr_ref[...] = _rope_pairs(h[:, MLA_Q_RANK + MLA_KV_RANK:], cos4, sin4s).astype(BF16)
    qn_ref[...] = _dot(c_q, w_qn_ref[...]).astype(BF16)
    qr_ref[...] = _rope_pairs(_dot(c_q, w_qr_ref[...]), cos4, sin4s).astype(BF16)
    kn_ref[...] = _dot(c_kv, w_uk_ref[...]).astype(BF16)
    vt_ref[...] = _dot_nt(w_uvt_ref[...], c_kv).astype(BF16)


def _mla_proj(x, mod, layer, positions3, freq4, w):
    b, s, d = x.shape
    tm = TOKEN_TILE
    hn = MLA_HEADS * MLA_NOPE_DIM
    hr = MLA_HEADS * MLA_ROPE_DIM
    hv = MLA_HEADS * MLA_V_DIM
    tok = lambda n: pl.BlockSpec((None, tm, n), lambda i, j: (i, j, 0))
    return pl.pallas_call(
        _mla_proj_kernel,
        grid=(b, s // tm),
        in_specs=[
            tok(d),
            pl.BlockSpec((None, None, 6, d), lambda i, j: (layer, i, 0, 0)),
            tok(1),
            _const_spec((1, LANES)),
            _const_spec(w["w_in"].shape),
            _const_spec((1, MLA_Q_RANK)),
            _const_spec((1, MLA_KV_RANK)),
            _const_spec(w["w_qn"].shape),
            _const_spec(w["w_qr"].shape),
            _const_spec(w["w_uk"].shape),
            _const_spec(w["w_uvt"].shape),
        ],
        out_specs=[
            tok(hn), tok(hr), tok(hn), tok(LANES),
            pl.BlockSpec((None, hv, tm), lambda i, j: (i, 0, j)),
        ],
        out_shape=[
            jax.ShapeDtypeStruct((b, s, hn), BF16),
            jax.ShapeDtypeStruct((b, s, hr), BF16),
            jax.ShapeDtypeStruct((b, s, hn), BF16),
            jax.ShapeDtypeStruct((b, s, LANES), BF16),
            jax.ShapeDtypeStruct((b, hv, s), BF16),
        ],
        compiler_params=_params("parallel", "parallel"),
        name="mla_proj",
    )(x, mod, positions3, freq4, w["w_in"], w["g_q"], w["g_kv"],
      w["w_qn"], w["w_qr"], w["w_uk"], w["w_uvt"])


def _flash_transposed(q_tiles, k_ref, vt_list, scale, seq, out_cb):
    t = ATTN_TILE
    n_heads = len(vt_list)
    row = lax.broadcasted_iota(jnp.int32, (t, t), 0)
    col = lax.broadcasted_iota(jnp.int32, (t, t), 1)
    causal = row <= col

    def step(q, vt_ref, k0, carry, mask):
        m, l, acc = carry
        st = _dot_nt(k_ref[pl.ds(k0, t), :], q)
        if scale != 1.0:
            st = st * scale
        if mask:
            st = jnp.where(causal, st, -jnp.inf)
        m_new = jnp.maximum(m, jnp.max(st, axis=0, keepdims=True))
        alpha = jnp.exp(m - m_new)
        p = jnp.exp(st - m_new)
        l = alpha * l + jnp.sum(p, axis=0, keepdims=True)
        acc = alpha * acc + _dot(vt_ref[:, pl.ds(k0, t)], p.astype(BF16))
        return m_new, l, acc

    for qi in range(seq // t):
        qs = q_tiles(qi)
        outs = []
        for h in range(n_heads):
            vt_ref = vt_list[h]
            dv = vt_ref.shape[0]
            init = (jnp.full((1, t), -jnp.inf, F32), jnp.zeros((1, t), F32),
                    jnp.zeros((dv, t), F32))
            carry = step(qs[h], vt_ref, qi * t, init, True)
            if qi > 0:
                carry = lax.fori_loop(
                    0, qi,
                    lambda j, c: step(qs[h], vt_ref, pl.multiple_of(j * t, t), c, False),
                    carry)
            m, l, acc = carry
            outs.append(acc / l)
        out_cb(qi, outs)


def _mla_attn_kernel(qn_ref, qr_ref, kn_ref, kr_ref, vt_ref, o_ref, kcat_ref, *, seq, scale):
    t = ATTN_TILE
    kcat_ref[:, :LANES] = kn_ref[...]
    kcat_ref[:, LANES:] = kr_ref[...]
    lane = lax.broadcasted_iota(jnp.int32, (1, LANES), 1)
    own = (((lane // ROPE_HALF) % 2) == (pl.program_id(1) % 2)).astype(F32).astype(BF16)

    def q_tiles(qi):
        rows = slice(qi * t, (qi + 1) * t)
        return [jnp.concatenate([qn_ref[rows, :], qr_ref[rows, :] * own], axis=1)]

    def out_cb(qi, outs):
        o_ref[qi * t:(qi + 1) * t, :] = outs[0].T.astype(o_ref.dtype)

    _flash_transposed(q_tiles, kcat_ref, [vt_ref], scale, seq, out_cb)


def _mla_attn(qn, qr, kn, kr, vt):
    b, s, _ = qn.shape
    dv = MLA_V_DIM
    scale = float((MLA_NOPE_DIM + MLA_ROPE_DIM) ** -0.5)
    head = lambda i, h: (i, 0, h)
    return pl.pallas_call(
        functools.partial(_mla_attn_kernel, seq=s, scale=scale),
        grid=(b, MLA_HEADS),
        in_specs=[
            pl.BlockSpec((None, s, LANES), head),
            pl.BlockSpec((None, s, LANES), lambda i, h: (i, 0, h // 2)),
            pl.BlockSpec((None, s, LANES), head),
            pl.BlockSpec((None, s, LANES), lambda i, h: (i, 0, 0)),
            pl.BlockSpec((None, dv, s), lambda i, h: (i, h, 0)),
        ],
        out_specs=pl.BlockSpec((None, s, dv), head),
        out_shape=jax.ShapeDtypeStruct((b, s, MLA_HEADS * dv), BF16),
        scratch_shapes=[pltpu.VMEM((s, 2 * LANES), BF16)],
        compiler_params=_params("parallel", "parallel"),
        name="mla_attn",
    )(qn, qr, kn, kr, vt)


def _fox_proj_kernel(x_ref, mod_ref, w_q_ref, w_k_ref, w_vt_ref, w_f_ref, b_f_ref, tri_ref,
                     q_ref, k_ref, vt_ref, qa_ref, ka_ref, carry_ref, *, scale):
    @pl.when(pl.program_id(1) == 0)
    def _():
        carry_ref[...] = jnp.zeros_like(carry_ref)

    x = x_ref[...]
    u = (x * (1.0 + mod_ref[1:2, :]) + mod_ref[0:1, :]).astype(BF16)
    q_ref[...] = (_dot(u, w_q_ref[...]) * scale).astype(BF16)
    k_ref[...] = _dot(u, w_k_ref[...]).astype(BF16)
    vt_ref[...] = _dot_nt(w_vt_ref[...], u).astype(BF16)

    z = _dot(u, w_f_ref[...]) + b_f_ref[...]
    log_f = -(jnp.maximum(-z, 0.0) + jnp.log1p(jnp.exp(-jnp.abs(z))))

    tri3 = tri_ref[...]
    carry = carry_ref[0:1, :]
    blocks = []
    for r in range(x.shape[0] // CUMSUM_BLOCK):
        hi, mid, lo = _split3(log_f[r * CUMSUM_BLOCK:(r + 1) * CUMSUM_BLOCK, :])
        cum = carry + _dot(tri3, jnp.concatenate([hi, mid, lo], axis=0))
        blocks.append(cum)
        carry = cum[CUMSUM_BLOCK - 1:CUMSUM_BLOCK, :]
    carry_ref[0:1, :] = carry
    f_cum = jnp.concatenate(blocks, axis=0)

    hi, mid, lo = (p.astype(F32) for p in _split3(f_cum))
    grp = lax.broadcasted_iota(jnp.int32, (1, LANES), 1) // FOX_GROUP
    qa_ref[...] = jnp.where(grp == 0, hi, jnp.where(grp == 1, mid, jnp.where(
        grp == 2, lo, jnp.where(grp < 6, 1.0, 0.0)))).astype(BF16)
    ka_ref[...] = jnp.where(grp < 3, 1.0, jnp.where(grp == 3, -hi, jnp.where(
        grp == 4, -mid, jnp.where(grp == 5, -lo, 0.0)))).astype(BF16)


def _fox_proj(x, mod, layer, w):
    b, s, d = x.shape
    tm = TOKEN_TILE
    tok = lambda n: pl.BlockSpec((None, tm, n), lambda i, j: (i, j, 0))
    return pl.pallas_call(
        functools.partial(_fox_proj_kernel, scale=float(FOX_HEAD_DIM ** -0.5)),
        grid=(b, s // tm),
        in_specs=[
            tok(d),
            pl.BlockSpec((None, None, 6, d), lambda i, j: (layer, i, 0, 0)),
            _const_spec(w["w_q"].shape),
            _const_spec(w["w_k"].shape),
            _const_spec(w["w_vt"].shape),
            _const_spec(w["w_f"].shape),
            _const_spec((1, LANES)),
            _const_spec(w["tri3"].shape),
        ],
        out_specs=[
            tok(d), tok(d),
            pl.BlockSpec((None, d, tm), lambda i, j: (i, 0, j)),
            tok(LANES), tok(LANES),
        ],
        out_shape=[
            jax.ShapeDtypeStruct((b, s, d), BF16),
            jax.ShapeDtypeStruct((b, s, d), BF16),
            jax.ShapeDtypeStruct((b, d, s), BF16),
            jax.ShapeDtypeStruct((b, s, LANES), BF16),
            jax.ShapeDtypeStruct((b, s, LANES), BF16),
        ],
        scratch_shapes=[pltpu.VMEM((8, LANES), F32)],
        compiler_params=_params("parallel", "arbitrary"),
        name="fox_proj",
    )(x, mod, w["w_q"], w["w_k"], w["w_vt"], w["w_f"], w["b_f"], w["tri3"])


def _fox_attn_kernel(q_ref, k_ref, qa_ref, ka_ref, vt_ref, o_ref, kcat_ref, *, seq):
    t = ATTN_TILE
    hd = FOX_HEAD_DIM
    kcat_ref[:, :LANES] = k_ref[...]
    kcat_ref[:, LANES:] = ka_ref[...]
    lane = lax.broadcasted_iota(jnp.int32, (1, LANES), 1)
    pair = pl.program_id(1)
    head_mask = [((lane // hd) == hh).astype(F32).astype(BF16) for hh in range(2)]
    bias_mask = [((lane % FOX_GROUP) == 2 * pair + hh).astype(F32).astype(BF16)
                 for hh in range(2)]

    def q_tiles(qi):
        rows = slice(qi * t, (qi + 1) * t)
        q2 = q_ref[rows, :]
        qa = qa_ref[rows, :]
        return [jnp.concatenate([q2 * head_mask[hh], qa * bias_mask[hh]], axis=1)
                for hh in range(2)]

    def out_cb(qi, outs):
        o_ref[qi * t:(qi + 1) * t, :] = jnp.concatenate(outs, axis=0).T.astype(o_ref.dtype)

    vt_list = [vt_ref.at[hh * hd:(hh + 1) * hd, :] for hh in range(2)]
    _flash_transposed(q_tiles, kcat_ref, vt_list, 1.0, seq, out_cb)


def _fox_attn(q, k, qa, ka, vt):
    b, s, d = q.shape
    pair = lambda i, g: (i, 0, g)
    shared = lambda i, g: (i, 0, 0)
    return pl.pallas_call(
        functools.partial(_fox_attn_kernel, seq=s),
        grid=(b, FOX_PAIRS),
        in_specs=[
            pl.BlockSpec((None, s, LANES), pair),
            pl.BlockSpec((None, s, LANES), pair),
            pl.BlockSpec((None, s, LANES), shared),
            pl.BlockSpec((None, s, LANES), shared),
            pl.BlockSpec((None, LANES, s), lambda i, g: (i, g, 0)),
        ],
        out_specs=pl.BlockSpec((None, s, LANES), pair),
        out_shape=jax.ShapeDtypeStruct((b, s, d), BF16),
        scratch_shapes=[pltpu.VMEM((s, 2 * LANES), BF16)],
        compiler_params=_params("parallel", "parallel"),
        name="fox_attn",
    )(q, k, qa, ka, vt)


def _post_kernel(a_ref, x_ref, mod_ref, wo_ref, wg_ref, wu_ref, wd_ref, lng_ref, lnb_ref,
                 o_ref, h_ref, *, alpha):
    x = x_ref[...]
    y = _dot(a_ref[...], wo_ref[...])
    x1 = _layer_norm(alpha * x + (1.0 + mod_ref[2:3, :]) * y, lng_ref[0:1, :], lnb_ref[0:1, :])
    u = (x1 * (1.0 + mod_ref[4:5, :]) + mod_ref[3:4, :]).astype(BF16)
    for c in range(wg_ref.shape[1] // FFN_CHUNK):
        cols = slice(c * FFN_CHUNK, (c + 1) * FFN_CHUNK)
        g = _dot(u, wg_ref[:, cols])
        up = _dot(u, wu_ref[:, cols])
        h_ref[:, cols] = (g * _sigmoid(g) * up).astype(BF16)
    y2 = _dot(h_ref[...], wd_ref[...])
    o_ref[...] = _layer_norm(alpha * x1 + (1.0 + mod_ref[5:6, :]) * y2,
                             lng_ref[1:2, :], lnb_ref[1:2, :])


def _post(attn, x, mod, layer, w, alpha):
    b, s, d = x.shape
    tm = TOKEN_TILE
    dff = w["w_gate"].shape[1]
    tok = lambda n: pl.BlockSpec((None, tm, n), lambda i, j: (i, j, 0))
    return pl.pallas_call(
        functools.partial(_post_kernel, alpha=alpha),
        grid=(b, s // tm),
        in_specs=[
            tok(d), tok(d),
            pl.BlockSpec((None, None, 6, d), lambda i, j: (layer, i, 0, 0)),
            _const_spec((d, d)),
            _const_spec((d, dff)),
            _const_spec((d, dff)),
            _const_spec((dff, d)),
            _const_spec((2, d)),
            _const_spec((2, d)),
        ],
        out_specs=tok(d),
        out_shape=jax.ShapeDtypeStruct((b, s, d), F32),
        scratch_shapes=[pltpu.VMEM((tm, dff), BF16)],
        compiler_params=_params("parallel", "parallel"),
        name="post",
    )(attn, x, mod, w["w_o"], w["w_gate"], w["w_up"], w["w_down"], w["ln_g"], w["ln_b"])


def _mla_weights(w_in, g_q, w_uq, g_kv, w_uk, w_uv):
    qk = MLA_Q_RANK + MLA_KV_RANK
    r1 = w_in[:, qk:qk + ROPE_HALF]
    r2 = w_in[:, qk + ROPE_HALF:]
    w_in_p = jnp.concatenate([w_in[:, :qk], r1, r1, r2, r2], axis=1)
    w_uq_h = w_uq.reshape(MLA_Q_RANK, MLA_HEADS, MLA_NOPE_DIM + MLA_ROPE_DIM)
    w_qn = w_uq_h[:, :, :MLA_NOPE_DIM].reshape(MLA_Q_RANK, -1)
    rope = w_uq_h[:, :, MLA_NOPE_DIM:].reshape(MLA_Q_RANK, MLA_HEADS // 2, 2, 2, ROPE_HALF)
    w_qr = rope.transpose(0, 1, 3, 2, 4).reshape(MLA_Q_RANK, -1)
    return {
        "w_in": w_in_p.astype(BF16), "g_q": g_q.reshape(1, -1), "g_kv": g_kv.reshape(1, -1),
        "w_qn": w_qn.astype(BF16), "w_qr": w_qr.astype(BF16),
        "w_uk": w_uk.astype(BF16), "w_uvt": w_uv.T.astype(BF16),
    }


def _fox_weights(w_in, b_f):
    d = D_MODEL
    reps = 6
    pad = LANES - reps * FOX_GROUP
    w_f = jnp.pad(jnp.tile(w_in[:, 3 * d:], (1, reps)), ((0, 0), (0, pad)))
    b_f6 = jnp.pad(jnp.tile(b_f, reps), (0, pad)).reshape(1, LANES)
    tri = jnp.tril(jnp.ones((CUMSUM_BLOCK, CUMSUM_BLOCK), BF16))
    return {
        "w_q": w_in[:, :d].astype(BF16), "w_k": w_in[:, d:2 * d].astype(BF16),
        "w_vt": w_in[:, 2 * d:3 * d].T.astype(BF16), "w_f": w_f.astype(BF16),
        "b_f": b_f6, "tri3": jnp.concatenate([tri, tri, tri], axis=1),
    }


def kernel(x, c, positions, mla_w_in, mla_g_q, mla_w_uq, mla_g_kv, mla_w_uk, mla_w_uv, mla_w_o,
           fox_w_in, fox_b_f, fox_w_o, ada_w, ada_b, ffn_w_gate, ffn_w_up, ffn_w_down,
           ln_g, ln_b):
    depth = ada_w.shape[0]
    b, s, d = x.shape
    alpha = float((2.0 * depth) ** 0.25)

    mod = _adaln(c, ada_w, ada_b).reshape(depth, b, 6, d)
    positions3 = positions.reshape(b, s, 1)
    inv_freq = ROPE_THETA ** (-jnp.arange(ROPE_HALF, dtype=F32) / ROPE_HALF)
    freq4 = jnp.tile(inv_freq, LANES // ROPE_HALF).reshape(1, LANES)

    for i in range(depth):
        j = i // N_MIXERS
        if i % N_MIXERS == 0:
            w = _mla_weights(mla_w_in[j], mla_g_q[j], mla_w_uq[j], mla_g_kv[j],
                             mla_w_uk[j], mla_w_uv[j])
            qn, qr, kn, kr, vt = _mla_proj(x, mod, i, positions3, freq4, w)
            attn = _mla_attn(qn, qr, kn, kr, vt)
            w_o = mla_w_o[j]
        else:
            w = _fox_weights(fox_w_in[j], fox_b_f[j])
            q, k, vt, qa, ka = _fox_proj(x, mod, i, w)
            attn = _fox_attn(q, k, qa, ka, vt)
            w_o = fox_w_o[j]
        wp = {"w_o": w_o.astype(BF16), "w_gate": ffn_w_gate[i].astype(BF16),
              "w_up": ffn_w_up[i].astype(BF16), "w_down": ffn_w_down[i].astype(BF16),
              "ln_g": ln_g[i], "ln_b": ln_b[i]}
        x = _post(attn, x, mod, i, wp, alpha)
    return x
```

```python
import functools
import math

import jax
import jax.numpy as jnp
from jax import lax
from jax.experimental import pallas as pl
from jax.experimental.pallas import tpu as pltpu

D_MODEL = 1024
MLA_HEADS = 8
MLA_NOPE_DIM = 128
MLA_ROPE_DIM = 64
MLA_V_DIM = 128
MLA_Q_RANK = 256
MLA_KV_RANK = 256
ROPE_THETA = 10000.0
FOX_HEADS = 16
FOX_HEAD_DIM = D_MODEL // FOX_HEADS
N_MIXERS = 2
NORM_EPS = 1e-5

LANES = 128
ROPE_HALF = MLA_ROPE_DIM // 2
FOX_GROUP = 16
HEADS_PER_STEP = 2

TOKEN_TILE = 512
ATTN_TILE = 512
FFN_CHUNK = 256
CUMSUM_BLOCK = 128
VMEM_LIMIT = 56 * 1024 * 1024

F32 = jnp.float32
BF16 = jnp.bfloat16
LOG2E = math.log2(math.e)


def _dot(a, b):
    return jnp.dot(a, b, preferred_element_type=F32)


def _dot_nt(a, b):
    return lax.dot_general(a, b, (((1,), (1,)), ((), ())), preferred_element_type=F32)


def _layer_norm(z, g, b):
    mu = jnp.mean(z, axis=-1, keepdims=True)
    zc = z - mu
    var = jnp.mean(zc * zc, axis=-1, keepdims=True)
    return zc * lax.rsqrt(var + NORM_EPS) * g + b


def _rms_norm(h, g):
    return h * lax.rsqrt(jnp.mean(h * h, axis=-1, keepdims=True) + NORM_EPS) * g


def _sigmoid(x):
    return 1.0 / (1.0 + jnp.exp(-x))


def _split3(x):
    hi = x.astype(BF16)
    r1 = x - hi.astype(F32)
    mid = r1.astype(BF16)
    lo = (r1 - mid.astype(F32)).astype(BF16)
    return hi, mid, lo


def _const_spec(shape):
    nd = len(shape)
    return pl.BlockSpec(shape, lambda *_: (0,) * nd, pipeline_mode=pl.Buffered(1))


def _params(*sem):
    return pltpu.CompilerParams(dimension_semantics=sem, vmem_limit_bytes=VMEM_LIMIT)


def _adaln_kernel(c_ref, w_ref, b_ref, o_ref):
    c = c_ref[...]
    c_act = (c * _sigmoid(c)).astype(BF16)
    o_ref[...] = _dot(c_act, w_ref[...].astype(BF16)) + b_ref[...]


def _adaln(c, ada_w, ada_b):
    depth, d, n = ada_w.shape
    b = c.shape[0]
    tn = 1024
    return pl.pallas_call(
        _adaln_kernel,
        grid=(depth, n // tn),
        in_specs=[
            pl.BlockSpec((b, d), lambda i, j: (0, 0)),
            pl.BlockSpec((None, d, tn), lambda i, j: (i, 0, j)),
            pl.BlockSpec((None, 1, tn), lambda i, j: (i, 0, j)),
        ],
        out_specs=pl.BlockSpec((None, b, tn), lambda i, j: (i, 0, j)),
        out_shape=jax.ShapeDtypeStruct((depth, b, n), F32),
        compiler_params=_params("parallel", "parallel"),
        name="adaln",
    )(c, ada_w, ada_b.reshape(depth, 1, n))


def _rope_pairs(x, cos4, sin4s):
    blocks = []
    for j in range(x.shape[1] // LANES):
        xb = x[:, j * LANES:(j + 1) * LANES]
        blocks.append(xb * cos4 + pltpu.roll(xb, LANES // 2, 1) * sin4s)
    return blocks[0] if len(blocks) == 1 else jnp.concatenate(blocks, axis=1)


def _mla_proj_kernel(x_ref, mod_ref, pos_ref, freq_ref, w_in_ref, gq_ref, gkv_ref,
                     w_qn_ref, w_qr_ref, w_uk_ref, w_uvt_ref,
                     qn_ref, qr_ref, kn_ref, kr_ref, vt_ref):
    x = x_ref[...]
    u = (x * (1.0 + mod_ref[1:2, :]) + mod_ref[0:1, :]).astype(BF16)
    h = _dot(u, w_in_ref[...])
    c_q = _rms_norm(h[:, :MLA_Q_RANK], gq_ref[...]).astype(BF16)
    c_kv = _rms_norm(h[:, MLA_Q_RANK:MLA_Q_RANK + MLA_KV_RANK], gkv_ref[...]).astype(BF16)

    ang = pos_ref[...].astype(F32) * freq_ref[...]
    cos4 = jnp.cos(ang)
    sin = jnp.sin(ang)
    lane = lax.broadcasted_iota(jnp.int32, (1, LANES), 1)
    sin4s = jnp.where(lane < LANES // 2, -sin, sin)

    kr_ref[...] = _rope_pairs(h[:, MLA_Q_RANK + MLA_KV_RANK:], cos4, sin4s).astype(BF16)
    qn_ref[...] = _dot(c_q, w_qn_ref[...]).astype(BF16)
    qr_ref[...] = _rope_pairs(_dot(c_q, w_qr_ref[...]), cos4, sin4s).astype(BF16)
    kn_ref[...] = _dot(c_kv, w_uk_ref[...]).astype(BF16)
    vt_ref[...] = _dot_nt(w_uvt_ref[...], c_kv).astype(BF16)


def _mla_proj(x, mod, layer, positions3, freq4, w):
    b, s, d = x.shape
    tm = TOKEN_TILE
    hn = MLA_HEADS * MLA_NOPE_DIM
    hr = MLA_HEADS * MLA_ROPE_DIM
    hv = MLA_HEADS * MLA_V_DIM
    tok = lambda n: pl.BlockSpec((None, tm, n), lambda i, j: (i, j, 0))
    return pl.pallas_call(
        _mla_proj_kernel,
        grid=(b, s // tm),
        in_specs=[
            tok(d),
            pl.BlockSpec((None, None, 6, d), lambda i, j: (layer, i, 0, 0)),
            tok(1),
            _const_spec((1, LANES)),
            _const_spec(w["w_in"].shape),
            _const_spec((1, MLA_Q_RANK)),
            _const_spec((1, MLA_KV_RANK)),
            _const_spec(w["w_qn"].shape),
            _const_spec(w["w_qr"].shape),
            _const_spec(w["w_uk"].shape),
            _const_spec(w["w_uvt"].shape),
        ],
        out_specs=[
            tok(hn), tok(hr), tok(hn), tok(LANES),
            pl.BlockSpec((None, hv, tm), lambda i, j: (i, 0, j)),
        ],
        out_shape=[
            jax.ShapeDtypeStruct((b, s, hn), BF16),
            jax.ShapeDtypeStruct((b, s, hr), BF16),
            jax.ShapeDtypeStruct((b, s, hn), BF16),
            jax.ShapeDtypeStruct((b, s, LANES), BF16),
            jax.ShapeDtypeStruct((b, hv, s), BF16),
        ],
        compiler_params=_params("parallel", "parallel"),
        name="mla_proj",
    )(x, mod, positions3, freq4, w["w_in"], w["g_q"], w["g_kv"],
      w["w_qn"], w["w_qr"], w["w_uk"], w["w_uvt"])


def _flash_transposed(q_tiles, k_refs, vt_refs, coef, seq, out_cb):
    t = ATTN_TILE
    row = lax.broadcasted_iota(jnp.int32, (t, t), 0)
    col = lax.broadcasted_iota(jnp.int32, (t, t), 1)
    causal = row <= col

    for qi in range(seq // t):
        qs = q_tiles(qi)
        n = qi * t
        outs = []
        for h in range(len(vt_refs)):
            k_ref, vt_ref = k_refs[h], vt_refs[h]
            sd = jnp.where(causal, _dot_nt(k_ref[n:n + t, :], qs[h]) * coef, -jnp.inf)
            m = jnp.max(sd, axis=0, keepdims=True)
            if qi > 0:
                sf = _dot_nt(k_ref[0:n, :], qs[h]) * coef
                m = jnp.maximum(m, jnp.max(sf, axis=0, keepdims=True))
            pd = jnp.exp2(sd - m)
            l = jnp.sum(pd, axis=0, keepdims=True)
            acc = _dot(vt_ref[:, n:n + t], pd.astype(BF16))
            if qi > 0:
                pf = jnp.exp2(sf - m)
                l = l + jnp.sum(pf, axis=0, keepdims=True)
                acc = acc + _dot(vt_ref[:, 0:n], pf.astype(BF16))
            outs.append(acc / l)
        out_cb(qi, outs)


def _mla_attn_kernel(qn_ref, qr_ref, kn_ref, kr_ref, vt_ref, o_ref, kcat_ref, *, seq, coef):
    t = ATTN_TILE
    nd, dv = MLA_NOPE_DIM, MLA_V_DIM
    lane = lax.broadcasted_iota(jnp.int32, (1, LANES), 1)
    for hh in range(HEADS_PER_STEP):
        kcat_ref[hh, :, :nd] = kn_ref[:, hh * nd:(hh + 1) * nd]
        kcat_ref[hh, :, nd:] = kr_ref[...]
    own = [(((lane // ROPE_HALF) % 2) == hh).astype(F32).astype(BF16)
           for hh in range(HEADS_PER_STEP)]

    def q_tiles(qi):
        rows = slice(qi * t, (qi + 1) * t)
        qr = qr_ref[rows, :]
        return [jnp.concatenate([qn_ref[rows, hh * nd:(hh + 1) * nd], qr * own[hh]], axis=1)
                for hh in range(HEADS_PER_STEP)]

    def out_cb(qi, outs):
        o_ref[qi * t:(qi + 1) * t, :] = jnp.concatenate(outs, axis=0).T.astype(o_ref.dtype)

    k_refs = [kcat_ref.at[hh] for hh in range(HEADS_PER_STEP)]
    vt_refs = [vt_ref.at[hh * dv:(hh + 1) * dv, :] for hh in range(HEADS_PER_STEP)]
    _flash_transposed(q_tiles, k_refs, vt_refs, coef, seq, out_cb)


def _mla_attn(qn, qr, kn, kr, vt):
    b, s, _ = qn.shape
    hp = HEADS_PER_STEP
    nd, dv = MLA_NOPE_DIM, MLA_V_DIM
    coef = float((MLA_NOPE_DIM + MLA_ROPE_DIM) ** -0.5 * LOG2E)
    pair = lambda i, g: (i, 0, g)
    return pl.pallas_call(
        functools.partial(_mla_attn_kernel, seq=s, coef=coef),
        grid=(b, MLA_HEADS // hp),
        in_specs=[
            pl.BlockSpec((None, s, hp * nd), pair),
            pl.BlockSpec((None, s, LANES), pair),
            pl.BlockSpec((None, s, hp * nd), pair),
            pl.BlockSpec((None, s, LANES), lambda i, g: (i, 0, 0)),
            pl.BlockSpec((None, hp * dv, s), lambda i, g: (i, g, 0)),
        ],
        out_specs=pl.BlockSpec((None, s, hp * dv), pair),
        out_shape=jax.ShapeDtypeStruct((b, s, MLA_HEADS * dv), BF16),
        scratch_shapes=[pltpu.VMEM((hp, s, nd + LANES), BF16)],
        compiler_params=_params("parallel", "parallel"),
        name="mla_attn",
    )(qn, qr, kn, kr, vt)


def _fox_proj_kernel(x_ref, mod_ref, w_q_ref, w_k_ref, w_vt_ref, w_f_ref, b_f_ref, tri_ref,
                     q_ref, k_ref, vt_ref, qa_ref, ka_ref, carry_ref, *, scale):
    @pl.when(pl.program_id(1) == 0)
    def _():
        carry_ref[...] = jnp.zeros_like(carry_ref)

    x = x_ref[...]
    u = (x * (1.0 + mod_ref[1:2, :]) + mod_ref[0:1, :]).astype(BF16)
    q_ref[...] = (_dot(u, w_q_ref[...]) * scale).astype(BF16)
    k_ref[...] = _dot(u, w_k_ref[...]).astype(BF16)
    vt_ref[...] = _dot_nt(w_vt_ref[...], u).astype(BF16)

    z = _dot(u, w_f_ref[...]) + b_f_ref[...]
    log_f = -(jnp.maximum(-z, 0.0) + jnp.log1p(jnp.exp(-jnp.abs(z))))

    tri3 = tri_ref[...]
    carry = carry_ref[0:1, :]
    blocks = []
    for r in range(x.shape[0] // CUMSUM_BLOCK):
        hi, mid, lo = _split3(log_f[r * CUMSUM_BLOCK:(r + 1) * CUMSUM_BLOCK, :])
        cum = carry + _dot(tri3, jnp.concatenate([hi, mid, lo], axis=0))
        blocks.append(cum)
        carry = cum[CUMSUM_BLOCK - 1:CUMSUM_BLOCK, :]
    carry_ref[0:1, :] = carry
    f_cum = jnp.concatenate(blocks, axis=0)

    hi, mid, lo = (p.astype(F32) for p in _split3(f_cum))
    grp = lax.broadcasted_iota(jnp.int32, (1, LANES), 1) // FOX_GROUP
    qa_ref[...] = jnp.where(grp == 0, hi, jnp.where(grp == 1, mid, jnp.where(
        grp == 2, lo, jnp.where(grp < 6, 1.0, 0.0)))).astype(BF16)
    ka_ref[...] = jnp.where(grp < 3, 1.0, jnp.where(grp == 3, -hi, jnp.where(
        grp == 4, -mid, jnp.where(grp == 5, -lo, 0.0)))).astype(BF16)


def _fox_proj(x, mod, layer, w):
    b, s, d = x.shape
    tm = TOKEN_TILE
    tok = lambda n: pl.BlockSpec((None, tm, n), lambda i, j: (i, j, 0))
    return pl.pallas_call(
        functools.partial(_fox_proj_kernel, scale=float(FOX_HEAD_DIM ** -0.5)),
        grid=(b, s // tm),
        in_specs=[
            tok(d),
            pl.BlockSpec((None, None, 6, d), lambda i, j: (layer, i, 0, 0)),
            _const_spec(w["w_q"].shape),
            _const_spec(w["w_k"].shape),
            _const_spec(w["w_vt"].shape),
            _const_spec(w["w_f"].shape),
            _const_spec((1, LANES)),
            _const_spec(w["tri3"].shape),
        ],
        out_specs=[
            tok(d), tok(d),
            pl.BlockSpec((None, d, tm), lambda i, j: (i, 0, j)),
            tok(LANES), tok(LANES),
        ],
        out_shape=[
            jax.ShapeDtypeStruct((b, s, d), BF16),
            jax.ShapeDtypeStruct((b, s, d), BF16),
            jax.ShapeDtypeStruct((b, d, s), BF16),
            jax.ShapeDtypeStruct((b, s, LANES), BF16),
            jax.ShapeDtypeStruct((b, s, LANES), BF16),
        ],
        scratch_shapes=[pltpu.VMEM((8, LANES), F32)],
        compiler_params=_params("parallel", "arbitrary"),
        name="fox_proj",
    )(x, mod, w["w_q"], w["w_k"], w["w_vt"], w["w_f"], w["b_f"], w["tri3"])


def _fox_attn_kernel(q_ref, k_ref, qa_ref, ka_ref, vt_ref, o_ref, kcat_ref, *, seq):
    t = ATTN_TILE
    hd = FOX_HEAD_DIM
    kcat_ref[:, :LANES] = k_ref[...]
    kcat_ref[:, LANES:] = ka_ref[...]
    lane = lax.broadcasted_iota(jnp.int32, (1, LANES), 1)
    pair = pl.program_id(1)
    head_mask = [((lane // hd) == hh).astype(F32).astype(BF16) for hh in range(HEADS_PER_STEP)]
    bias_mask = [((lane % FOX_GROUP) == HEADS_PER_STEP * pair + hh).astype(F32).astype(BF16)
                 for hh in range(HEADS_PER_STEP)]

    def q_tiles(qi):
        rows = slice(qi * t, (qi + 1) * t)
        q2 = q_ref[rows, :]
        qa = qa_ref[rows, :]
        return [jnp.concatenate([q2 * head_mask[hh], qa * bias_mask[hh]], axis=1)
                for hh in range(HEADS_PER_STEP)]

    def out_cb(qi, outs):
        o_ref[qi * t:(qi + 1) * t, :] = jnp.concatenate(outs, axis=0).T.astype(o_ref.dtype)

    vt_refs = [vt_ref.at[hh * hd:(hh + 1) * hd, :] for hh in range(HEADS_PER_STEP)]
    _flash_transposed(q_tiles, [kcat_ref] * HEADS_PER_STEP, vt_refs, LOG2E, seq, out_cb)


def _fox_attn(q, k, qa, ka, vt):
    b, s, d = q.shape
    pair = lambda i, g: (i, 0, g)
    shared = lambda i, g: (i, 0, 0)
    return pl.pallas_call(
        functools.partial(_fox_attn_kernel, seq=s),
        grid=(b, FOX_HEADS // HEADS_PER_STEP),
        in_specs=[
            pl.BlockSpec((None, s, LANES), pair),
            pl.BlockSpec((None, s, LANES), pair),
            pl.BlockSpec((None, s, LANES), shared),
            pl.BlockSpec((None, s, LANES), shared),
            pl.BlockSpec((None, LANES, s), lambda i, g: (i, g, 0)),
        ],
        out_specs=pl.BlockSpec((None, s, LANES), pair),
        out_shape=jax.ShapeDtypeStruct((b, s, d), BF16),
        scratch_shapes=[pltpu.VMEM((s, 2 * LANES), BF16)],
        compiler_params=_params("parallel", "parallel"),
        name="fox_attn",
    )(q, k, qa, ka, vt)


def _post_kernel(a_ref, x_ref, mod_ref, wo_ref, wg_ref, wu_ref, wd_ref, lng_ref, lnb_ref,
                 o_ref, h_ref, *, alpha):
    x = x_ref[...]
    y = _dot(a_ref[...], wo_ref[...])
    x1 = _layer_norm(alpha * x + (1.0 + mod_ref[2:3, :]) * y, lng_ref[0:1, :], lnb_ref[0:1, :])
    u = (x1 * (1.0 + mod_ref[4:5, :]) + mod_ref[3:4, :]).astype(BF16)
    for c in range(wg_ref.shape[1] // FFN_CHUNK):
        cols = slice(c * FFN_CHUNK, (c + 1) * FFN_CHUNK)
        g = _dot(u, wg_ref[:, cols])
        up = _dot(u, wu_ref[:, cols])
        h_ref[:, cols] = (g * _sigmoid(g) * up).astype(BF16)
    y2 = _dot(h_ref[...], wd_ref[...])
    o_ref[...] = _layer_norm(alpha * x1 + (1.0 + mod_ref[5:6, :]) * y2,
                             lng_ref[1:2, :], lnb_ref[1:2, :])


def _post(attn, x, mod, layer, w, alpha):
    b, s, d = x.shape
    tm = TOKEN_TILE
    dff = w["w_gate"].shape[1]
    tok = lambda n: pl.BlockSpec((None, tm, n), lambda i, j: (i, j, 0))
    return pl.pallas_call(
        functools.partial(_post_kernel, alpha=alpha),
        grid=(b, s // tm),
        in_specs=[
            tok(d), tok(d),
            pl.BlockSpec((None, None, 6, d), lambda i, j: (layer, i, 0, 0)),
            _const_spec((d, d)),
            _const_spec((d, dff)),
            _const_spec((d, dff)),
            _const_spec((dff, d)),
            _const_spec((2, d)),
            _const_spec((2, d)),
        ],
        out_specs=tok(d),
        out_shape=jax.ShapeDtypeStruct((b, s, d), F32),
        scratch_shapes=[pltpu.VMEM((tm, dff), BF16)],
        compiler_params=_params("parallel", "parallel"),
        name="post",
    )(attn, x, mod, w["w_o"], w["w_gate"], w["w_up"], w["w_down"], w["ln_g"], w["ln_b"])


def _mla_weights(w_in, g_q, w_uq, g_kv, w_uk, w_uv):
    qk = MLA_Q_RANK + MLA_KV_RANK
    r1 = w_in[:, qk:qk + ROPE_HALF]
    r2 = w_in[:, qk + ROPE_HALF:]
    w_in_p = jnp.concatenate([w_in[:, :qk], r1, r1, r2, r2], axis=1)
    w_uq_h = w_uq.reshape(MLA_Q_RANK, MLA_HEADS, MLA_NOPE_DIM + MLA_ROPE_DIM)
    w_qn = w_uq_h[:, :, :MLA_NOPE_DIM].reshape(MLA_Q_RANK, -1)
    rope = w_uq_h[:, :, MLA_NOPE_DIM:].reshape(MLA_Q_RANK, MLA_HEADS // 2, 2, 2, ROPE_HALF)
    w_qr = rope.transpose(0, 1, 3, 2, 4).reshape(MLA_Q_RANK, -1)
    return {
        "w_in": w_in_p.astype(BF16), "g_q": g_q.reshape(1, -1), "g_kv": g_kv.reshape(1, -1),
        "w_qn": w_qn.astype(BF16), "w_qr": w_qr.astype(BF16),
        "w_uk": w_uk.astype(BF16), "w_uvt": w_uv.T.astype(BF16),
    }


def _fox_weights(w_in, b_f):
    d = D_MODEL
    reps = 6
    pad = LANES - reps * FOX_GROUP
    w_f = jnp.pad(jnp.tile(w_in[:, 3 * d:], (1, reps)), ((0, 0), (0, pad)))
    b_f6 = jnp.pad(jnp.tile(b_f, reps), (0, pad)).reshape(1, LANES)
    tri = jnp.tril(jnp.ones((CUMSUM_BLOCK, CUMSUM_BLOCK), BF16))
    return {
        "w_q": w_in[:, :d].astype(BF16), "w_k": w_in[:, d:2 * d].astype(BF16),
        "w_vt": w_in[:, 2 * d:3 * d].T.astype(BF16), "w_f": w_f.astype(BF16),
        "b_f": b_f6, "tri3": jnp.concatenate([tri, tri, tri], axis=1),
    }


def kernel(x, c, positions, mla_w_in, mla_g_q, mla_w_uq, mla_g_kv, mla_w_uk, mla_w_uv, mla_w_o,
           fox_w_in, fox_b_f, fox_w_o, ada_w, ada_b, ffn_w_gate, ffn_w_up, ffn_w_down,
           ln_g, ln_b):
    depth = ada_w.shape[0]
    b, s, d = x.shape
    alpha = float((2.0 * depth) ** 0.25)

    mod = _adaln(c, ada_w, ada_b).reshape(depth, b, 6, d)
    positions3 = positions.reshape(b, s, 1)
    inv_freq = ROPE_THETA ** (-jnp.arange(ROPE_HALF, dtype=F32) / ROPE_HALF)
    freq4 = jnp.tile(inv_freq, LANES // ROPE_HALF).reshape(1, LANES)

    for i in range(depth):
        j = i // N_MIXERS
        if i % N_MIXERS == 0:
            w = _mla_weights(mla_w_in[j], mla_g_q[j], mla_w_uq[j], mla_g_kv[j],
                             mla_w_uk[j], mla_w_uv[j])
            qn, qr, kn, kr, vt = _mla_proj(x, mod, i, positions3, freq4, w)
            attn = _mla_attn(qn, qr, kn, kr, vt)
            w_o = mla_w_o[j]
        else:
            w = _fox_weights(fox_w_in[j], fox_b_f[j])
            q, k, vt, qa, ka = _fox_proj(x, mod, i, w)
            attn = _fox_attn(q, k, qa, ka, vt)
            w_o = fox_w_o[j]
        wp = {"w_o": w_o.astype(BF16), "w_gate": ffn_w_gate[i].astype(BF16),
              "w_up": ffn_w_up[i].astype(BF16), "w_down": ffn_w_down[i].astype(BF16),
              "ln_g": ln_g[i], "ln_b": ln_b[i]}
        x = _post(attn, x, mod, i, wp, alpha)
    return x
```
